```python
import math
import jax, jax.numpy as jnp
from jax import lax
import numpy as np

D_MODEL = 1024
BATCH = 4
SEQ = 4096
DEPTH = 1
DEC_BATCH = 16
DEC_SEQ = 16
PAST_LEN = 1024

CHUNK = 64
Q_BLOCK = 128
RET_HEADS = 8
RET_HD = 64
FOX_HEADS = 8
FOX_HD = 64
D_RET = RET_HEADS * RET_HD
D_FOX = FOX_HEADS * FOX_HD
D_MIX = D_RET + D_FOX
IN_COLS = 4 * D_RET + 3 * D_FOX + FOX_HEADS
SPLITS = [D_RET, 2 * D_RET, 3 * D_RET, 4 * D_RET, 4 * D_RET + D_FOX, 4 * D_RET + 2 * D_FOX, 4 * D_RET + 3 * D_FOX]
ROPE_BASE = 10000.0
N_KEYS = 128
N_EXPERTS = N_KEYS * N_KEYS
PEER_HEADS = 8
PEER_DKEY = 256
PEER_HALF = PEER_DKEY // 2
PEER_TOPK = 16
PEER_BLOCK = 128
EPS = 1e-6

kernel_name = 'retention_fox_peer_streaming_step'

F32 = jnp.float32


def rms_norm(x, g):
    xf = x.astype(F32)
    y = xf * lax.rsqrt(jnp.mean(xf * xf, axis=-1, keepdims=True) + EPS)
    return (y * g.astype(F32)).astype(x.dtype)


def rotary(x, pos):
    half = x.shape[-1] // 2
    inv = ROPE_BASE ** (-jnp.arange(half, dtype=F32) / half)
    ang = pos.astype(F32)[:, None] * inv[None, :]
    cos = jnp.cos(ang)[None, :, None, :]
    sin = jnp.sin(ang)[None, :, None, :]
    x1 = x[..., :half].astype(F32)
    x2 = x[..., half:].astype(F32)
    return jnp.concatenate([x1 * cos - x2 * sin, x1 * sin + x2 * cos], axis=-1)


def project(h, w_in, b_f, fox_qn_g, fox_kn_g, pos):
    b, t, _ = h.shape
    z = jnp.einsum('btd,dc->btc', h, w_in)
    rq, rk, rv, rg, fq, fk, fv, ff = jnp.split(z, SPLITS, axis=-1)
    rq = rotary(rq.reshape(b, t, RET_HEADS, RET_HD), pos)
    rk = rotary(rk.reshape(b, t, RET_HEADS, RET_HD), pos) * (RET_HD ** -0.5)
    rv = rv.reshape(b, t, RET_HEADS, RET_HD).astype(F32)
    fq = rms_norm(fq.reshape(b, t, FOX_HEADS, FOX_HD), fox_qn_g)
    fk = rms_norm(fk.reshape(b, t, FOX_HEADS, FOX_HD), fox_kn_g)
    fv = fv.reshape(b, t, FOX_HEADS, FOX_HD)
    logf = jax.nn.log_sigmoid(ff.astype(F32) + b_f.astype(F32))
    return rq, rk, rv, rg, fq, fk, fv, logf


def retention(q, k, v, s0):
    b, t, h, _ = q.shape
    blen = min(CHUNK, t)
    n = t // blen
    lg = jnp.log1p(-(2.0 ** (-5.0 - jnp.arange(h, dtype=F32))))
    idx = jnp.arange(blen, dtype=F32)
    dist = idx[:, None] - idx[None, :]
    intra = jnp.where(dist[None] >= 0, jnp.exp(lg[:, None, None] * jnp.maximum(dist, 0.0)[None]), 0.0)
    q_dec = jnp.exp(lg[None, :] * (idx[:, None] + 1.0))
    k_dec = jnp.exp(lg[None, :] * (blen - 1.0 - idx[:, None]))
    s_dec = jnp.exp(lg * blen)

    def to_chunks(a):
        return a.reshape(b, n, blen, h, a.shape[-1]).transpose(1, 0, 2, 3, 4)

    def step(s, inp):
        qc, kc, vc = inp
        sc = jnp.einsum('blhd,bmhd->bhlm', qc, kc) * intra[None]
        o = (jnp.einsum('bhlm,bmhe->blhe', sc, vc)
             + jnp.einsum('blhd,bhde->blhe', qc, s) * q_dec[None, :, :, None])
        s = s * s_dec[None, :, None, None] + jnp.einsum('blhd,blhe->bhde', kc * k_dec[None, :, :, None], vc)
        return s, o

    s, o = lax.scan(step, s0, (to_chunks(q), to_chunks(k), to_chunks(v)))
    o = o.transpose(1, 0, 2, 3, 4).reshape(b, t, h, -1)
    return o, s


def fox_block(q, k, v, cq, ck, qpos, kpos):
    logits = jnp.einsum('bqhd,bkhd->bhqk', q, k).astype(F32) * (FOX_HD ** -0.5)
    logits = logits + (cq.transpose(0, 2, 1)[..., :, None] - ck.transpose(0, 2, 1)[..., None, :])
    mask = kpos[None, :] <= qpos[:, None]
    logits = jnp.where(mask[None, None], logits, -jnp.inf)
    p = jax.nn.softmax(logits, axis=-1)
    return jnp.einsum('bhqk,bkhd->bqhd', p.astype(v.dtype), v)


def fox_prompt(q, k, v, logf):
    b, t, h, d = q.shape
    c = jnp.cumsum(logf, axis=1)
    pos = jnp.arange(t)
    nb = t // Q_BLOCK
    qb = q.reshape(b, nb, Q_BLOCK, h, d).transpose(1, 0, 2, 3, 4)
    cqb = c.reshape(b, nb, Q_BLOCK, h).transpose(1, 0, 2, 3)
    pb = pos.reshape(nb, Q_BLOCK)
    out = lax.map(lambda a: fox_block(a[0], k, v, a[1], c, a[2], pos), (qb, cqb, pb))
    return out.transpose(1, 0, 2, 3, 4).reshape(b, t, h, d)


def fox_sample(q, k, v, logf, ck, cv, clogf):
    p = ck.shape[1]
    t = q.shape[1]
    k_all = jnp.concatenate([ck.astype(k.dtype), k], axis=1)
    v_all = jnp.concatenate([cv.astype(v.dtype), v], axis=1)
    c = jnp.cumsum(jnp.concatenate([clogf.astype(F32), logf], axis=1), axis=1)
    kpos = jnp.arange(p + t)
    qpos = p + jnp.arange(t)
    return fox_block(q, k_all, v_all, c[:, p:], c, qpos, kpos)


def combine(ro, rg, fo, ret_out_g, fox_out_g, w_out, dtype):
    b, t = rg.shape[:2]
    r = rms_norm(ro, ret_out_g).reshape(b, t, D_RET) * jax.nn.silu(rg.astype(F32))
    f = rms_norm(fo, fox_out_g).reshape(b, t, D_FOX)
    m = jnp.concatenate([r.astype(dtype), f.astype(dtype)], axis=-1)
    return jnp.einsum('btc,cd->btd', m, w_out)


def peer_tokens(x, wq, keys, u, v):
    n = x.shape[0]
    q = jnp.einsum('nd,dc->nc', x, wq).reshape(n, PEER_HEADS, 2, PEER_HALF)
    s1 = jnp.einsum('nhd,hkd->nhk', q[:, :, 0], keys[0]).astype(F32)
    s2 = jnp.einsum('nhd,hkd->nhk', q[:, :, 1], keys[1]).astype(F32)
    v1, i1 = lax.top_k(s1, PEER_TOPK)
    v2, i2 = lax.top_k(s2, PEER_TOPK)
    cand = (v1[..., :, None] + v2[..., None, :]).reshape(n, PEER_HEADS, PEER_TOPK * PEER_TOPK)
    cv, ci = lax.top_k(cand, PEER_TOPK)
    e = (jnp.take_along_axis(i1, ci // PEER_TOPK, axis=-1) * N_KEYS
         + jnp.take_along_axis(i2, ci % PEER_TOPK, axis=-1))
    g = jax.nn.softmax(cv, axis=-1)
    ue = u[e]
    ve = v[e]
    act = jax.nn.gelu(jnp.einsum('nd,nhkd->nhk', x, ue).astype(F32))
    w = (g * act).astype(x.dtype)
    return jnp.einsum('nhk,nhkd->nd', w, ve)


def peer(x, wq, keys, u, v):
    b, t, d = x.shape
    flat = x.reshape(-1, d)
    n = flat.shape[0]
    if n % PEER_BLOCK == 0 and n > PEER_BLOCK:
        blocks = flat.reshape(n // PEER_BLOCK, PEER_BLOCK, d)
        out = lax.map(lambda xb: peer_tokens(xb, wq, keys, u, v), blocks)
    else:
        out = peer_tokens(flat, wq, keys, u, v)
    return out.reshape(b, t, d)


def layer(x, pos, s0, past, norm1_g, w_in, b_f, fox_qn_g, fox_kn_g, ret_out_g, fox_out_g,
          w_out, norm2_g, peer_wq, peer_keys, peer_u, peer_v):
    h = rms_norm(x, norm1_g)
    rq, rk, rv, rg, fq, fk, fv, logf = project(h, w_in, b_f, fox_qn_g, fox_kn_g, pos)
    ro, s_new = retention(rq, rk, rv, s0)
    if past is None:
        fo = fox_prompt(fq, fk, fv, logf)
    else:
        fo = fox_sample(fq, fk, fv, logf, past[0], past[1], past[2])
    x = x + combine(ro, rg, fo, ret_out_g, fox_out_g, w_out, x.dtype)
    x = x + peer(rms_norm(x, norm2_g), peer_wq, peer_keys, peer_u, peer_v)
    return x, fk, fv, logf, s_new.astype(x.dtype)


def setup_inputs(seed: int = 0) -> dict:
    key = jax.random.key(seed)
    ks = jax.random.split(key, 20)
    nrm = jax.random.normal
    return {
        'x_prompt': nrm(ks[0], (BATCH, SEQ, D_MODEL), F32),
        'x_sample': nrm(ks[1], (DEC_BATCH, DEC_SEQ, D_MODEL), F32),
        'cache_fox_k': nrm(ks[2], (DEPTH, DEC_BATCH, PAST_LEN, FOX_HEADS, FOX_HD), F32),
        'cache_fox_v': nrm(ks[3], (DEPTH, DEC_BATCH, PAST_LEN, FOX_HEADS, FOX_HD), F32),
        'cache_fox_logf': jax.nn.log_sigmoid(2.0 + nrm(ks[4], (DEPTH, DEC_BATCH, PAST_LEN, FOX_HEADS), F32)),
        'state_ret': 0.5 * nrm(ks[5], (DEPTH, DEC_BATCH, RET_HEADS, RET_HD, RET_HD), F32),
        'norm1_g': 1.0 + 0.02 * nrm(ks[6], (DEPTH, D_MODEL), F32),
        'w_in': nrm(ks[7], (DEPTH, D_MODEL, IN_COLS), F32) * (D_MODEL ** -0.5),
        'b_f': 2.0 + 0.5 * nrm(ks[8], (DEPTH, FOX_HEADS), F32),
        'fox_qn_g': 1.0 + 0.02 * nrm(ks[9], (DEPTH, FOX_HEADS, FOX_HD), F32),
        'fox_kn_g': 1.0 + 0.02 * nrm(ks[10], (DEPTH, FOX_HEADS, FOX_HD), F32),
        'ret_out_g': 1.0 + 0.02 * nrm(ks[11], (DEPTH, RET_HEADS, RET_HD), F32),
        'fox_out_g': 1.0 + 0.02 * nrm(ks[12], (DEPTH, FOX_HEADS, FOX_HD), F32),
        'w_out': nrm(ks[13], (DEPTH, D_MIX, D_MODEL), F32) * (D_MIX ** -0.5),
        'norm2_g': 1.0 + 0.02 * nrm(ks[14], (DEPTH, D_MODEL), F32),
        'peer_wq': nrm(ks[15], (DEPTH, D_MODEL, PEER_HEADS * PEER_DKEY), F32) * (D_MODEL ** -0.5),
        'peer_keys': nrm(ks[16], (DEPTH, 2, PEER_HEADS, N_KEYS, PEER_HALF), F32) * (PEER_HALF ** -0.5),
        'peer_u': nrm(ks[17], (DEPTH, N_EXPERTS, D_MODEL), F32) * (D_MODEL ** -0.5),
        'peer_v': nrm(ks[18], (DEPTH, N_EXPERTS, D_MODEL), F32) * (PEER_HEADS ** -0.5),
    }


def reference(x_prompt, x_sample, cache_fox_k, cache_fox_v, cache_fox_logf, state_ret,
              norm1_g, w_in, b_f, fox_qn_g, fox_kn_g, ret_out_g, fox_out_g, w_out, norm2_g,
              peer_wq, peer_keys, peer_u, peer_v):
    y_p = x_prompt
    y_s = x_sample
    pos_p = jnp.arange(x_prompt.shape[1])
    pos_s = PAST_LEN + jnp.arange(x_sample.shape[1])
    kp_l, vp_l, lp_l, sp_l = [], [], [], []
    ks_l, vs_l, ls_l, ss_l = [], [], [], []
    for l in range(DEPTH):
        w = (norm1_g[l], w_in[l], b_f[l], fox_qn_g[l], fox_kn_g[l], ret_out_g[l], fox_out_g[l],
             w_out[l], norm2_g[l], peer_wq[l], peer_keys[l], peer_u[l], peer_v[l])
        s0_p = jnp.zeros((x_prompt.shape[0], RET_HEADS, RET_HD, RET_HD), F32)
        y_p, kp, vp, lp, sp = layer(y_p, pos_p, s0_p, None, *w)
        past = (cache_fox_k[l], cache_fox_v[l], cache_fox_logf[l])
        y_s, kk, vv, ll, ss = layer(y_s, pos_s, state_ret[l].astype(F32), past, *w)
        kp_l.append(kp); vp_l.append(vp); lp_l.append(lp); sp_l.append(sp)
        ks_l.append(kk); vs_l.append(vv); ls_l.append(ll); ss_l.append(ss)
    new_fox_k_p = jnp.stack(kp_l)
    new_fox_v_p = jnp.stack(vp_l)
    new_fox_logf_p = jnp.stack(lp_l)
    new_state_ret_p = jnp.stack(sp_l)
    new_fox_k_s = jnp.stack(ks_l)
    new_fox_v_s = jnp.stack(vs_l)
    new_fox_logf_s = jnp.stack(ls_l)
    new_state_ret_s = jnp.stack(ss_l)
    return (y_p, y_s, new_fox_k_p, new_fox_v_p, new_fox_logf_p, new_state_ret_p,
            new_fox_k_s, new_fox_v_s, new_fox_logf_s, new_state_ret_s)
```

```python
import functools

import jax
import jax.numpy as jnp
from jax import lax
from jax.experimental import pallas as pl
from jax.experimental.pallas import tpu as pltpu

F32 = jnp.float32
BF16 = jnp.bfloat16

D_MODEL = 1024
HEADS = 8
HD = 64
D_HEADS = HEADS * HD
PAST_LEN = 1024
ROPE_BASE = 10000.0
N_KEYS = 128
N_EXPERTS = N_KEYS * N_KEYS
PEER_HALF = 128
TOPK = 16
EPS = 1e-6
NEG_INF = float("-inf")

IN_COLS = 7 * D_HEADS + HEADS
IN_COLS_PAD = 7 * D_HEADS + 128
QK_AUG = 128

LANES = 128
VMEM_LIMIT = 56 * 1024 * 1024

RET_CHUNK = 128
GELU_C = 0.7978845608028654


def _cparams(sem):
    return pltpu.CompilerParams(dimension_semantics=sem, vmem_limit_bytes=VMEM_LIMIT)


def _proj_kernel(x_ref, g1_ref, w_ref, bf_ref, qg_ref, kg_ref, cos_ref, sin_ref, ones_ref,
                 rq_ref, rk_ref, rv_ref, rg_ref, fq_ref, fk_ref, fv_ref, lf_ref):
    x = x_ref[...]
    ms = jnp.mean(x * x, axis=-1, keepdims=True)
    h = (x * lax.rsqrt(ms + EPS) * g1_ref[...]).astype(BF16)

    def proj(group, width=D_HEADS):
        c0 = group * D_HEADS
        return jnp.dot(h, w_ref[:, c0:c0 + width], preferred_element_type=F32)

    cos = jnp.concatenate([cos_ref[...]] * (D_HEADS // LANES), axis=1)
    sin = jnp.concatenate([sin_ref[...]] * (D_HEADS // LANES), axis=1)
    lane = lax.broadcasted_iota(jnp.int32, cos.shape, 1)
    first_half = (lane & (HD // 2)) == 0

    def rotary(z):
        partner = jnp.where(first_half, pltpu.roll(z, D_HEADS - HD // 2, 1), pltpu.roll(z, HD // 2, 1))
        return z * cos + partner * sin

    rq_ref[...] = rotary(proj(0)).astype(BF16)
    rk_ref[...] = rotary(proj(1)) * (HD ** -0.5)
    rv_ref[...] = proj(2).astype(BF16)
    rg_ref[...] = proj(3)

    ones_bd = ones_ref[...]

    def head_norm(z, g):
        zz = z * z
        hi = zz.astype(BF16)
        lo = (zz - hi.astype(F32)).astype(BF16)
        ss = (jnp.dot(hi, ones_bd, preferred_element_type=F32)
              + jnp.dot(lo, ones_bd, preferred_element_type=F32))
        return z * lax.rsqrt(ss * (1.0 / HD) + EPS) * g

    fq_ref[...] = head_norm(proj(4), qg_ref[...]) * (HD ** -0.5)
    fk_ref[...] = head_norm(proj(5), kg_ref[...])
    fv_ref[...] = proj(6)
    ff = proj(7, LANES)[:, :HEADS] + bf_ref[...]
    lf_ref[...] = jnp.minimum(ff, 0.0) - jnp.log1p(jnp.exp(-jnp.abs(ff)))


def _proj(x, g1, w_in_p, b_f, qn_g, kn_g, cos_t, sin_t, ones_bd, tm):
    n = x.shape[0]
    nblk = n // tm
    tab_blocks = cos_t.shape[0] // tm
    row = lambda i: (i, 0)
    fixed = lambda i: (0, 0)
    tab = lambda i: (i % tab_blocks, 0)
    wide = pl.BlockSpec((tm, D_HEADS), row)
    return pl.pallas_call(
        _proj_kernel,
        grid=(nblk,),
        in_specs=[
            pl.BlockSpec((tm, D_MODEL), row),
            pl.BlockSpec((1, D_MODEL), fixed),
            pl.BlockSpec((D_MODEL, IN_COLS_PAD), fixed),
            pl.BlockSpec((1, HEADS), fixed),
            pl.BlockSpec((1, D_HEADS), fixed),
            pl.BlockSpec((1, D_HEADS), fixed),
            pl.BlockSpec((tm, LANES), tab),
            pl.BlockSpec((tm, LANES), tab),
            pl.BlockSpec((D_HEADS, D_HEADS), fixed),
        ],
        out_specs=[wide, wide, wide, wide, wide, wide, wide, pl.BlockSpec((tm, HEADS), row)],
        out_shape=[
            jax.ShapeDtypeStruct((n, D_HEADS), BF16),
            jax.ShapeDtypeStruct((n, D_HEADS), F32),
            jax.ShapeDtypeStruct((n, D_HEADS), BF16),
            jax.ShapeDtypeStruct((n, D_HEADS), F32),
            jax.ShapeDtypeStruct((n, D_HEADS), F32),
            jax.ShapeDtypeStruct((n, D_HEADS), F32),
            jax.ShapeDtypeStruct((n, D_HEADS), F32),
            jax.ShapeDtypeStruct((n, HEADS), F32),
        ],
        compiler_params=_cparams(("parallel",)),
        name="proj",
    )(x, g1, w_in_p, b_f, qn_g, kn_g, cos_t, sin_t, ones_bd)


def _cumsum_kernel(x_ref, o_ref):
    x = x_ref[...]
    t = x.shape[1]
    lane = lax.broadcasted_iota(jnp.int32, x.shape, 1)
    shift = 1
    while shift < t:
        x = x + jnp.where(lane >= shift, pltpu.roll(x, shift, 1), 0.0)
        shift *= 2
    o_ref[...] = x


def _cumsum_lanes(x):
    return pl.pallas_call(
        _cumsum_kernel,
        out_shape=jax.ShapeDtypeStruct(x.shape, F32),
        name="cumsum",
    )(x)


def _fox_kernel(qt_ref, k_ref, vt_ref, g_ref, o_ref, *, tq, tk, q_off, nk):
    qi = pl.program_id(1)
    qt = qt_ref[0]
    q_lo = q_off + qi * tq
    n_full = (q_lo + 1) // tk
    n_all = jnp.minimum((q_lo + tq + tk - 1) // tk, nk)

    def step(j, carry, masked):
        m, l, acc = carry
        s = jnp.dot(k_ref[0, j], qt, preferred_element_type=F32)
        if masked:
            kpos = j * tk + lax.broadcasted_iota(jnp.int32, (tk, tq), 0)
            qpos = q_lo + lax.broadcasted_iota(jnp.int32, (tk, tq), 1)
            s = jnp.where(kpos <= qpos, s, NEG_INF)
        m_new = jnp.maximum(m, jnp.max(s, axis=0, keepdims=True))
        alpha = jnp.exp(m - m_new)
        p = jnp.exp(s - m_new)
        l = alpha * l + jnp.sum(p, axis=0, keepdims=True)
        acc = alpha * acc + jnp.dot(vt_ref[0, j], p.astype(BF16), preferred_element_type=F32)
        return m_new, l, acc

    init = (jnp.full((1, tq), NEG_INF, F32), jnp.zeros((1, tq), F32), jnp.zeros((HD, tq), F32))
    carry = lax.fori_loop(0, n_full, functools.partial(step, masked=False), init)
    _, l, acc = lax.fori_loop(n_full, n_all, functools.partial(step, masked=True), carry)
    o = acc / l
    ms = jnp.mean(o * o, axis=0, keepdims=True)
    o_ref[0] = o * lax.rsqrt(ms + EPS) * g_ref[0]


def _fox(qt, k4, vt4, g_b, *, tq, tk, q_off):
    bh, _, tqtot = qt.shape
    nk = k4.shape[1]
    nq = tqtot // tq
    kern = functools.partial(_fox_kernel, tq=tq, tk=tk, q_off=q_off, nk=nk)
    return pl.pallas_call(
        kern,
        grid=(bh, nq),
        in_specs=[
            pl.BlockSpec((1, QK_AUG, tq), lambda b, i: (b, 0, i)),
            pl.BlockSpec((1, nk, tk, QK_AUG), lambda b, i: (b, 0, 0, 0)),
            pl.BlockSpec((1, nk, HD, tk), lambda b, i: (b, 0, 0, 0)),
            pl.BlockSpec((1, HD, tq), lambda b, i: (b % HEADS, 0, 0)),
        ],
        out_specs=pl.BlockSpec((1, HD, tq), lambda b, i: (b, 0, i)),
        out_shape=jax.ShapeDtypeStruct((bh, HD, tqtot), F32),
        compiler_params=_cparams(("parallel", "arbitrary")),
        name="fox",
    )(qt, k4, vt4, g_b)


def _split3(c):
    hi = c.astype(BF16)
    r1 = c - hi.astype(F32)
    mid = r1.astype(BF16)
    lo = (r1 - mid.astype(F32)).astype(BF16)
    return hi, mid, lo


def _fox_operands(fq, fk, fv, cq, ck, tk):
    b, tq_, h, d = fq.shape
    tk_ = fk.shape[1]
    qh, qm, ql = _split3(cq)
    kh, km, kl = _split3(-ck)
    ones_q = jnp.ones((b, h, 3, tq_), BF16)
    qt = jnp.concatenate([
        fq.astype(BF16).transpose(0, 2, 3, 1),
        jnp.stack([qh, qm, ql], axis=2), ones_q,
        jnp.zeros((b, h, QK_AUG - d - 6, tq_), BF16)], axis=2)
    ones_k = jnp.ones((b, h, tk_, 3), BF16)
    kaug = jnp.concatenate([
        fk.astype(BF16).transpose(0, 2, 1, 3),
        ones_k, jnp.stack([kh, km, kl], axis=3),
        jnp.zeros((b, h, tk_, QK_AUG - d - 6), BF16)], axis=3)
    nk = tk_ // tk
    k4 = kaug.reshape(b * h, nk, tk, QK_AUG)
    vt4 = fv.astype(BF16).reshape(b, nk, tk, h, d).transpose(0, 3, 1, 4, 2).reshape(b * h, nk, d, tk)
    return qt.reshape(b * h, QK_AUG, tq_), k4, vt4


def _ret_kernel(q_ref, k_ref, v_ref, s0_ref, intra_ref, qdec_ref, kdec_ref, sdec_ref, g_ref,
                o_ref, sout_ref, s_ref, *, nch):
    t = pl.program_id(1)

    @pl.when(t == 0)
    def _():
        s_ref[...] = s0_ref[0]

    ln = RET_CHUNK
    for c in range(nch):
        r0 = c * ln
        for h in range(HEADS):
            c0 = h * HD
            qh = q_ref[0, r0:r0 + ln, c0:c0 + HD]
            kh = k_ref[0, r0:r0 + ln, c0:c0 + HD]
            vh = v_ref[0, r0:r0 + ln, c0:c0 + HD]
            sc = lax.dot_general(qh, kh.astype(BF16), (((1,), (1,)), ((), ())),
                                 preferred_element_type=F32)
            scm = (sc * intra_ref[h]).astype(BF16)
            s_h = s_ref[h]
            o = (jnp.dot(scm, vh, preferred_element_type=F32)
                 + jnp.dot(qh, s_h.astype(BF16), preferred_element_type=F32) * qdec_ref[h])
            kd = (kh * kdec_ref[h]).T.astype(BF16)
            s_ref[h] = s_h * sdec_ref[h] + jnp.dot(kd, vh, preferred_element_type=F32)
            ms = jnp.mean(o * o, axis=-1, keepdims=True)
            o_ref[0, r0:r0 + ln, c0:c0 + HD] = o * lax.rsqrt(ms + EPS) * g_ref[h]

    @pl.when(t == pl.num_programs(1) - 1)
    def _():
        sout_ref[0] = s_ref[...]


def _retention(q, k, v, s0, dec, g_b, tc):
    b, t, _ = q.shape
    intra, qdec, kdec, sdec = dec
    nt = t // tc
    tok = pl.BlockSpec((1, tc, D_HEADS), lambda i, j: (i, j, 0))
    st = pl.BlockSpec((1, HEADS, HD, HD), lambda i, j: (i, 0, 0, 0))
    c3 = lambda shp: pl.BlockSpec(shp, lambda i, j: (0, 0, 0))
    kern = functools.partial(_ret_kernel, nch=tc // RET_CHUNK)
    return pl.pallas_call(
        kern,
        grid=(b, nt),
        in_specs=[tok, tok, tok, st,
                  c3((HEADS, RET_CHUNK, RET_CHUNK)), c3((HEADS, RET_CHUNK, HD)),
                  c3((HEADS, RET_CHUNK, HD)), c3((HEADS, HD, HD)), c3((HEADS, RET_CHUNK, HD))],
        out_specs=[tok, st],
        out_shape=[jax.ShapeDtypeStruct((b, t, D_HEADS), F32),
                   jax.ShapeDtypeStruct((b, HEADS, HD, HD), F32)],
        scratch_shapes=[pltpu.VMEM((HEADS, HD, HD), F32)],
        compiler_params=_cparams(("parallel", "arbitrary")),
        name="retention",
    )(q, k, v, s0, intra, qdec, kdec, sdec, g_b)


def _ret_decays(blen):
    ln = RET_CHUNK
    lg = jnp.log1p(-(2.0 ** (-5.0 - jnp.arange(HEADS, dtype=F32))))
    idx = jnp.arange(ln, dtype=F32)
    dist = idx[:, None] - idx[None, :]
    intra = jnp.where(dist[None] >= 0, jnp.exp(lg[:, None, None] * jnp.maximum(dist, 0.0)[None]), 0.0)
    qdec = jnp.exp(lg[:, None] * (idx[None, :] + 1.0))
    kdec = jnp.where(idx[None, :] < blen, jnp.exp(lg[:, None] * (blen - 1.0 - idx[None, :])), 0.0)
    sdec = jnp.exp(lg * blen)
    bc = lambda a: jnp.broadcast_to(a[:, :, None], (HEADS, ln, HD))
    return intra, bc(qdec), bc(kdec), jnp.broadcast_to(sdec[:, None, None], (HEADS, HD, HD))


def _combine_kernel(ro_ref, rg_ref, f_ref, x_ref, w_ref, g2_ref, x2_ref, xh_ref):
    rg = rg_ref[...]
    r = ro_ref[...] * (rg / (1.0 + jnp.exp(-rg)))
    m = jnp.concatenate([r, f_ref[...]], axis=1).astype(BF16)
    x2 = x_ref[...] + jnp.dot(m, w_ref[...], preferred_element_type=F32)
    x2_ref[...] = x2
    ms = jnp.mean(x2 * x2, axis=-1, keepdims=True)
    xh_ref[...] = (x2 * lax.rsqrt(ms + EPS) * g2_ref[...]).astype(BF16)


def _combine(ro, rg, f, x, w_out, g2, tm):
    n = x.shape[0]
    row = lambda i: (i, 0)
    fixed = lambda i: (0, 0)
    half = pl.BlockSpec((tm, D_HEADS), row)
    full = pl.BlockSpec((tm, D_MODEL), row)
    return pl.pallas_call(
        _combine_kernel,
        grid=(n // tm,),
        in_specs=[half, half, half, full,
                  pl.BlockSpec((2 * D_HEADS, D_MODEL), fixed), pl.BlockSpec((1, D_MODEL), fixed)],
        out_specs=[full, full],
        out_shape=[jax.ShapeDtypeStruct((n, D_MODEL), F32), jax.ShapeDtypeStruct((n, D_MODEL), BF16)],
        compiler_params=_cparams(("parallel",)),
        name="combine",
    )(ro, rg, f, x, w_out, g2)


EXPERT_BLOCK = 1024
K1_PER_BLOCK = EXPERT_BLOCK // N_KEYS


def _argmax_step(cur, iota_f, sentinel):
    m = jnp.max(cur, axis=0, keepdims=True)
    idx = jnp.min(jnp.where(cur == m, iota_f, sentinel), axis=0, keepdims=True)
    return m, iota_f == idx


def _peer_route(h, s_ref, cur_ref, rank_ref, vtop_ref, r2_ref, b_ref, c1_ref, a_ref):
    tn = cur_ref.shape[1]
    iota_k = lax.broadcasted_iota(jnp.int32, (N_KEYS, tn), 0).astype(F32)
    ranks = []
    for side in range(2):
        cur_ref[...] = s_ref[side * HEADS + h]
        rank_ref[...] = jnp.full((N_KEYS, tn), float(TOPK), F32)
        for it in range(TOPK):
            cur = cur_ref[...]
            m, sel = _argmax_step(cur, iota_k, float(N_KEYS))
            rank_ref[...] = jnp.where(sel, float(it), rank_ref[...])
            vtop_ref[side, it:it + 1, :] = m
            cur_ref[...] = jnp.where(sel, NEG_INF, cur)
        ranks.append(rank_ref[...])
    r1, r2 = ranks
    v1 = vtop_ref[0]
    v2 = vtop_ref[1]

    row8 = lax.broadcasted_iota(jnp.int32, (8, tn), 0)
    v2a, v2b = v2[0:8], v2[8:16]
    r = lambda i: v1[i:i + 1]
    t0 = r(0) + v2a
    t1 = r(0) + v2b
    t2 = r(1) + v2a
    t3 = jnp.where(row8 < 5, r(2) + v2a, NEG_INF)
    t4 = jnp.where(row8 < 4, r(3) + v2a,
                   jnp.where(row8 < 7, r(4) + pltpu.roll(v2a, 4, 0), NEG_INF))
    v1p = jnp.where(row8 < 2, r(5), jnp.where(row8 < 4, r(6), r(7)))
    v2p = jnp.where((row8 & 1) == 0, v2[0:1], v2[1:2])
    t5 = jnp.where(row8 < 6, v1p + v2p, NEG_INF)
    t6 = v1[8:16] + v2[0:1]
    cand0 = jnp.concatenate([t0, t1, t2, t3, t4, t5, t6], axis=0)
    iota_c = lax.broadcasted_iota(jnp.int32, cand0.shape, 0).astype(F32)
    cand = cand0
    selm = jnp.zeros(cand0.shape, F32)
    for _ in range(TOPK):
        _, sel = _argmax_step(cand, iota_c, float(cand0.shape[0]))
        selm = jnp.where(sel, 1.0, selm)
        cand = jnp.where(sel, NEG_INF, cand)
    z = jnp.sum(jnp.where(selm > 0.0, jnp.exp(cand0 - cand0[0:1]), 0.0), axis=0, keepdims=True)

    rs = lambda a: jnp.sum(a, axis=0, keepdims=True)
    s4, s5 = selm[32:40], selm[40:48]
    counts = [rs(selm[0:16]), rs(selm[16:24]), rs(selm[24:32]),
              rs(jnp.where(row8 < 4, s4, 0.0)), rs(jnp.where(row8 >= 4, s4, 0.0)),
              rs(jnp.where(row8 < 2, s5, 0.0)),
              rs(jnp.where((row8 >= 2) & (row8 < 4), s5, 0.0)),
              rs(jnp.where(row8 >= 4, s5, 0.0))]
    counts += [selm[48 + i:49 + i] for i in range(8)]
    c1 = jnp.zeros((N_KEYS, tn), F32)
    for i in range(TOPK):
        c1 = jnp.where(r1 == float(i), counts[i], c1)

    r2_ref[h] = r2
    c1_ref[h] = c1
    a_ref[h] = jnp.exp(s_ref[h] - v1[0:1])
    b_ref[h] = jnp.exp(s_ref[HEADS + h] - v2[0:1]) / z


def _peer_kernel(xt_ref, wqt_ref, keys_ref, u_ref, vt_ref, x2_ref, y_ref,
                 s_ref, cur_ref, rank_ref, vtop_ref, r2_ref, b_ref, c1_ref, a_ref, acc_ref, w_ref):
    j = pl.program_id(1)
    tn = xt_ref.shape[1]

    @pl.when(j == 0)
    def _():
        xt = xt_ref[...]
        for hs in range(2 * HEADS):
            h, side = hs // 2, hs % 2
            r0 = h * 2 * PEER_HALF + side * PEER_HALF
            qt = jnp.dot(wqt_ref[r0:r0 + PEER_HALF, :], xt, preferred_element_type=F32)
            s_ref[side * HEADS + h] = jnp.dot(keys_ref[side * HEADS + h], qt.astype(BF16),
                                              preferred_element_type=F32)

        def route(h, carry):
            _peer_route(h, s_ref, cur_ref, rank_ref, vtop_ref, r2_ref, b_ref, c1_ref, a_ref)
            return carry

        lax.fori_loop(0, HEADS, route, 0)
        acc_ref[...] = jnp.zeros(acc_ref.shape, F32)

    ht = jnp.dot(u_ref[...], xt_ref[...], preferred_element_type=F32)
    k1_base = pl.multiple_of(j * K1_PER_BLOCK, K1_PER_BLOCK)
    for s in range(K1_PER_BLOCK):
        gate = jnp.zeros((N_KEYS, tn), F32)
        for h in range(HEADS):
            c1row = c1_ref[h, pl.ds(k1_base, K1_PER_BLOCK), :][s:s + 1]
            arow = a_ref[h, pl.ds(k1_base, K1_PER_BLOCK), :][s:s + 1]
            gate = gate + jnp.where(r2_ref[h] < c1row, b_ref[h], 0.0) * arow
        hs_ = ht[s * N_KEYS:(s + 1) * N_KEYS]
        act = 0.5 * hs_ * (1.0 + jnp.tanh(GELU_C * (hs_ + 0.044715 * (hs_ * hs_ * hs_))))
        w_ref[s * N_KEYS:(s + 1) * N_KEYS, :] = (act * gate).astype(BF16)
    acc_ref[...] += jnp.dot(vt_ref[...], w_ref[...], preferred_element_type=F32)

    @pl.when(j == pl.num_programs(1) - 1)
    def _():
        y_ref[...] = x2_ref[...] + acc_ref[...].T


def _peer(xt, wqt, keys, u, vt, x2, tn):
    n = x2.shape[0]
    ne = N_EXPERTS // EXPERT_BLOCK
    return pl.pallas_call(
        _peer_kernel,
        grid=(n // tn, ne),
        in_specs=[
            pl.BlockSpec((D_MODEL, tn), lambda i, j: (0, i)),
            pl.BlockSpec((2 * HEADS * PEER_HALF, D_MODEL), lambda i, j: (0, 0)),
            pl.BlockSpec((2 * HEADS, N_KEYS, PEER_HALF), lambda i, j: (0, 0, 0)),
            pl.BlockSpec((EXPERT_BLOCK, D_MODEL), lambda i, j: (j, 0)),
            pl.BlockSpec((D_MODEL, EXPERT_BLOCK), lambda i, j: (0, j)),
            pl.BlockSpec((tn, D_MODEL), lambda i, j: (i, 0)),
        ],
        out_specs=pl.BlockSpec((tn, D_MODEL), lambda i, j: (i, 0)),
        out_shape=jax.ShapeDtypeStruct((n, D_MODEL), F32),
        scratch_shapes=[
            pltpu.VMEM((2 * HEADS, N_KEYS, tn), F32),
            pltpu.VMEM((N_KEYS, tn), F32),
            pltpu.VMEM((N_KEYS, tn), F32),
            pltpu.VMEM((2, TOPK, tn), F32),
            pltpu.VMEM((HEADS, N_KEYS, tn), F32),
            pltpu.VMEM((HEADS, N_KEYS, tn), F32),
            pltpu.VMEM((HEADS, N_KEYS, tn), F32),
            pltpu.VMEM((HEADS, N_KEYS, tn), F32),
            pltpu.VMEM((D_MODEL, tn), F32),
            pltpu.VMEM((EXPERT_BLOCK, tn), BF16),
        ],
        compiler_params=_cparams(("parallel", "arbitrary")),
        name="peer",
    )(xt, wqt, keys, u, vt, x2)


def _rotary_tables(pos):
    half = HD // 2
    inv = ROPE_BASE ** (-jnp.arange(half, dtype=F32) / half)
    ang = pos.astype(F32)[:, None] * inv[None, :]
    cos, sin = jnp.cos(ang), jnp.sin(ang)
    cos_t = jnp.concatenate([cos, cos, cos, cos], axis=1)
    sin_t = jnp.concatenate([-sin, sin, -sin, sin], axis=1)
    return cos_t, sin_t


def _layer(x, pos, s0, past, wts, *, tm, tn, fox_tq, fox_tk):
    b, t, _ = x.shape
    n = b * t
    xf = x.reshape(n, D_MODEL)
    cos_t, sin_t = _rotary_tables(pos)
    if t < tm:
        cos_t, sin_t = jnp.tile(cos_t, (tm // t, 1)), jnp.tile(sin_t, (tm // t, 1))
    rq, rk, rv, rg, fq, fk, fv, logf = _proj(
        xf, wts["g1"], wts["w_in"], wts["b_f"], wts["qn_g"], wts["kn_g"], cos_t, sin_t,
        wts["ones_bd"], tm)

    tpad = max(t, RET_CHUNK)
    pad_t = lambda a: jnp.pad(a.reshape(b, t, D_HEADS), ((0, 0), (0, tpad - t), (0, 0)))
    ro, s_new = _retention(pad_t(rq), pad_t(rk), pad_t(rv), s0, _ret_decays(float(min(RET_CHUNK, t))),
                           wts["ret_g_b"], min(tpad, 4 * RET_CHUNK))
    ro = ro[:, :t].reshape(n, D_HEADS)

    fq4 = fq.reshape(b, t, HEADS, HD)
    fk4 = fk.reshape(b, t, HEADS, HD)
    fv4 = fv.reshape(b, t, HEADS, HD)
    lf_t = logf.reshape(b, t, HEADS).transpose(0, 2, 1)
    if past is None:
        c = _cumsum_lanes(lf_t.reshape(b * HEADS, t)).reshape(b, HEADS, t)
        qt, k4, vt4 = _fox_operands(fq4, fk4, fv4, c, c, fox_tk)
        q_off, tq_tot = 0, t
    else:
        ck, cv, clf = past
        p = ck.shape[1]
        tk_tot = -(-(p + t) // fox_tk) * fox_tk
        lf_all = jnp.concatenate([clf.astype(F32).transpose(0, 2, 1), lf_t], axis=2)
        lf_all = jnp.pad(lf_all, ((0, 0), (0, 0), (0, tk_tot - p - t)))
        c = _cumsum_lanes(lf_all.reshape(b * HEADS, tk_tot)).reshape(b, HEADS, tk_tot)
        padk = lambda a, new: jnp.pad(jnp.concatenate([a.astype(F32), new], axis=1),
                                      ((0, 0), (0, tk_tot - p - t), (0, 0), (0, 0)))
        cq = jnp.pad(c[:, :, p:p + t], ((0, 0), (0, 0), (0, fox_tq - t)))
        fq_p = jnp.pad(fq4, ((0, 0), (0, fox_tq - t), (0, 0), (0, 0)))
        qt, k4, vt4 = _fox_operands(fq_p, padk(ck, fk4), padk(cv, fv4), cq, c, fox_tk)
        q_off, tq_tot = p, fox_tq
    ft = _fox(qt, k4, vt4, wts["fox_g_b"][:, :, :fox_tq], tq=fox_tq, tk=fox_tk, q_off=q_off)
    f = ft.reshape(b, HEADS, HD, tq_tot)[:, :, :, :t].transpose(0, 3, 1, 2).reshape(n, D_HEADS)

    x2, xh = _combine(ro, rg, f, xf, wts["w_out"], wts["g2"], tm)
    y = _peer(xh.T, wts["wq_t"], wts["keys"], wts["u"], wts["v_t"], x2, tn)
    return (y.reshape(b, t, D_MODEL), fk4, fv4, logf.reshape(b, t, HEADS), s_new)


def _prep_weights(l, norm1_g, w_in, b_f, fox_qn_g, fox_kn_g, ret_out_g, fox_out_g, w_out, norm2_g,
                  peer_wq, peer_keys, peer_u, peer_v, fox_tq_max):
    head_of_lane = jnp.arange(D_HEADS) // HD
    return {
        "g1": norm1_g[l].reshape(1, D_MODEL),
        "w_in": jnp.pad(w_in[l], ((0, 0), (0, IN_COLS_PAD - IN_COLS))).astype(BF16),
        "b_f": b_f[l].reshape(1, HEADS),
        "qn_g": fox_qn_g[l].reshape(1, D_HEADS),
        "kn_g": fox_kn_g[l].reshape(1, D_HEADS),
        "ones_bd": (head_of_lane[:, None] == head_of_lane[None, :]).astype(BF16),
        "ret_g_b": jnp.broadcast_to(ret_out_g[l][:, None, :], (HEADS, RET_CHUNK, HD)),
        "fox_g_b": jnp.broadcast_to(fox_out_g[l][:, :, None], (HEADS, HD, fox_tq_max)),
        "w_out": w_out[l].astype(BF16),
        "g2": norm2_g[l].reshape(1, D_MODEL),
        "wq_t": peer_wq[l].T.astype(BF16),
        "keys": peer_keys[l].reshape(2 * HEADS, N_KEYS, PEER_HALF).astype(BF16),
        "u": peer_u[l].astype(BF16),
        "v_t": peer_v[l].T.astype(BF16),
    }


def kernel(x_prompt, x_sample, cache_fox_k, cache_fox_v, cache_fox_logf, state_ret, norm1_g, w_in, b_f,
           fox_qn_g, fox_kn_g, ret_out_g, fox_out_g, w_out, norm2_g, peer_wq, peer_keys, peer_u, peer_v):
    depth = norm1_g.shape[0]
    y_p, y_s = x_prompt, x_sample
    pos_p = jnp.arange(x_prompt.shape[1])
    pos_s = PAST_LEN + jnp.arange(x_sample.shape[1])
    outs_p, outs_s = [], []
    for l in range(depth):
        wts = _prep_weights(l, norm1_g, w_in, b_f, fox_qn_g, fox_kn_g, ret_out_g, fox_out_g, w_out,
                            norm2_g, peer_wq, peer_keys, peer_u, peer_v, 256)
        s0_p = jnp.zeros((x_prompt.shape[0], HEADS, HD, HD), F32)
        y_p, kp, vp, lp, sp = _layer(y_p, pos_p, s0_p, None, wts, tm=512, tn=512, fox_tq=256, fox_tk=256)
        past = (cache_fox_k[l], cache_fox_v[l], cache_fox_logf[l])
        y_s, ks, vs, ls, ss = _layer(y_s, pos_s, state_ret[l].astype(F32), past, wts,
                                     tm=256, tn=256, fox_tq=128, fox_tk=128)
        outs_p.append((kp, vp, lp, sp))
        outs_s.append((ks, vs, ls, ss))
    stack = lambda outs, i: jnp.stack([o[i] for o in outs])
    return (y_p, y_s,
            stack(outs_p, 0), stack(outs_p, 1), stack(outs_p, 2), stack(outs_p, 3),
            stack(outs_s, 0), stack(outs_s, 1), stack(outs_s, 2), stack(outs_s, 3))
```
